```python
import jax, jax.numpy as jnp
from jax import lax
import numpy as np

D_MODEL = 2048
BATCH = 8
SEQ = 2048
DEPTH = 1

CHUNK = 64
D_MIX = D_MODEL
FOX_HEAD_DIM = 128
FOX_WIDTH = D_MIX // 2
FOX_HEADS = FOX_WIDTH // FOX_HEAD_DIM
Q_BLOCK = 128
MLSTM_HEAD_DIM = 256
MLSTM_WIDTH = D_MIX - FOX_WIDTH
MLSTM_HEADS = MLSTM_WIDTH // MLSTM_HEAD_DIM
CONV_K = 4
PEER_HEADS = 8
N_KEYS = 128
N_EXPERTS = N_KEYS * N_KEYS
PEER_TOPK = 16
PEER_QDIM = 256
PEER_TOKEN_BLOCK = 128
RMS_EPS = 1e-6
N_MOD = 6

kernel_name = "hybrid_fox_mlstm_peer_block"


def rms_norm(x, gain):
    xf = x.astype(jnp.float32)
    r = lax.rsqrt(jnp.mean(xf * xf, axis=-1, keepdims=True) + RMS_EPS)
    return (xf * r).astype(x.dtype) * gain


def causal_depthwise_conv(x, w, b):
    C = x.shape[-1]
    y = lax.conv_general_dilated(
        x, w[:, None, :].astype(x.dtype), window_strides=(1,),
        padding=[(CONV_K - 1, 0)], dimension_numbers=('NWC', 'WIO', 'NWC'),
        feature_group_count=C)
    return y + b


def fox_attention(q, k, v, f_pre, q_gain, k_gain):
    B, S, H, Dh = q.shape
    q = rms_norm(q, q_gain).transpose(0, 2, 1, 3)
    k = rms_norm(k, k_gain).transpose(0, 2, 1, 3)
    v = v.transpose(0, 2, 1, 3)
    F = jnp.cumsum(jax.nn.log_sigmoid(f_pre.astype(jnp.float32)), axis=1).transpose(0, 2, 1)
    scale = FOX_HEAD_DIM ** -0.5
    outs = []
    for blk in range(S // Q_BLOCK):
        lo, hi = blk * Q_BLOCK, (blk + 1) * Q_BLOCK
        s = jnp.einsum('bhqd,bhkd->bhqk', q[:, :, lo:hi], k[:, :, :hi]).astype(jnp.float32) * scale
        s = s + F[:, :, lo:hi, None] - F[:, :, None, :hi]
        causal = (lo + jnp.arange(Q_BLOCK))[:, None] >= jnp.arange(hi)[None, :]
        s = jnp.where(causal, s, -jnp.inf)
        p = jax.nn.softmax(s, axis=-1).astype(v.dtype)
        outs.append(jnp.einsum('bhqk,bhkd->bhqd', p, v[:, :, :hi]))
    o = jnp.concatenate(outs, axis=2)
    return o.transpose(0, 2, 1, 3).reshape(B, S, H * Dh)


def mlstm_chunkwise(q, k, v, i_pre, f_pre):
    B, S, H, Dh = q.shape
    NC = S // CHUNK

    def to_chunks(t):
        return t.reshape(B, NC, CHUNK, *t.shape[2:]).swapaxes(0, 1)

    k = k * (Dh ** -0.5)
    xs = (to_chunks(q.astype(jnp.float32)), to_chunks(k.astype(jnp.float32)),
          to_chunks(v.astype(jnp.float32)), to_chunks(i_pre.astype(jnp.float32)),
          to_chunks(jax.nn.log_sigmoid(f_pre.astype(jnp.float32))))
    causal = jnp.tril(jnp.ones((CHUNK, CHUNK), dtype=bool))

    def step(carry, inp):
        C, n, m = carry
        qc, kc, vc, ic, fc = inp
        b = jnp.cumsum(fc, axis=1).transpose(0, 2, 1)
        ic = ic.transpose(0, 2, 1)
        logD = b[:, :, :, None] - b[:, :, None, :] + ic[:, :, None, :]
        logD = jnp.where(causal, logD, -jnp.inf)
        m_inter = b + m[:, :, None]
        m_t = jnp.maximum(m_inter, jnp.max(logD, axis=-1))
        Dm = jnp.exp(logD - m_t[..., None])
        inter = jnp.exp(m_inter - m_t)
        W = jnp.einsum('blhd,bshd->bhls', qc, kc) * Dm
        num = (jnp.einsum('bhls,bshd->bhld', W, vc)
               + inter[..., None] * jnp.einsum('blhd,bhde->bhle', qc, C))
        den = jnp.sum(W, axis=-1) + inter * jnp.einsum('blhd,bhd->bhl', qc, n)
        h = num / jnp.maximum(jnp.abs(den), jnp.exp(-m_t))[..., None]
        bL = b[:, :, -1]
        w_s = bL[..., None] - b + ic
        m_new = jnp.maximum(bL + m, jnp.max(w_s, axis=-1))
        decay = jnp.exp(bL + m - m_new)
        ws = jnp.exp(w_s - m_new[..., None])
        C_new = decay[..., None, None] * C + jnp.einsum('bhs,bshd,bshe->bhde', ws, kc, vc)
        n_new = decay[..., None] * n + jnp.einsum('bhs,bshd->bhd', ws, kc)
        return (C_new, n_new, m_new), h.transpose(0, 2, 1, 3)

    init = (jnp.zeros((B, H, Dh, Dh), jnp.float32), jnp.zeros((B, H, Dh), jnp.float32),
            jnp.zeros((B, H), jnp.float32))
    _, hs = lax.scan(step, init, xs)
    return hs.swapaxes(0, 1).reshape(B, S, H, Dh).astype(q.dtype)


def mixer(h, w_in, fox_f_bias, fox_q_gain, fox_k_gain, mlstm_conv_w, mlstm_conv_b,
          mlstm_i_bias, mlstm_f_bias, mlstm_head_gain, w_out):
    B, S, _ = h.shape
    sizes = [FOX_WIDTH, FOX_WIDTH, FOX_WIDTH, FOX_HEADS,
             MLSTM_WIDTH, MLSTM_WIDTH, MLSTM_WIDTH, MLSTM_WIDTH, MLSTM_HEADS, MLSTM_HEADS]
    split_at = np.cumsum(sizes)[:-1].tolist()
    p = h @ w_in
    fq, fk, fv, ff, mq, mk, mv, mo, mi, mf = jnp.split(p, split_at, axis=-1)
    fox_out = fox_attention(fq.reshape(B, S, FOX_HEADS, FOX_HEAD_DIM),
                            fk.reshape(B, S, FOX_HEADS, FOX_HEAD_DIM),
                            fv.reshape(B, S, FOX_HEADS, FOX_HEAD_DIM),
                            ff + fox_f_bias, fox_q_gain, fox_k_gain)
    mqk = jax.nn.silu(causal_depthwise_conv(jnp.concatenate([mq, mk], axis=-1),
                                            mlstm_conv_w, mlstm_conv_b))
    mq, mk = jnp.split(mqk, 2, axis=-1)
    hm = mlstm_chunkwise(mq.reshape(B, S, MLSTM_HEADS, MLSTM_HEAD_DIM),
                         mk.reshape(B, S, MLSTM_HEADS, MLSTM_HEAD_DIM),
                         mv.reshape(B, S, MLSTM_HEADS, MLSTM_HEAD_DIM),
                         mi + mlstm_i_bias, mf + mlstm_f_bias)
    hm = rms_norm(hm, mlstm_head_gain.reshape(MLSTM_HEADS, MLSTM_HEAD_DIM))
    mlstm_out = hm.reshape(B, S, MLSTM_WIDTH) * jax.nn.sigmoid(mo)
    return jnp.concatenate([fox_out, mlstm_out], axis=-1) @ w_out


def peer(h, w_query, sub_keys_1, sub_keys_2, expert_down, expert_up):
    B, S, D = h.shape
    half = PEER_QDIM // 2
    q = (h @ w_query).reshape(B, S, PEER_HEADS, PEER_QDIM)
    s1 = jnp.einsum('bshd,nd->bshn', q[..., :half], sub_keys_1).astype(jnp.float32)
    s2 = jnp.einsum('bshd,nd->bshn', q[..., half:], sub_keys_2).astype(jnp.float32)
    v1, i1 = lax.top_k(s1, PEER_TOPK)
    v2, i2 = lax.top_k(s2, PEER_TOPK)
    cand = (v1[..., :, None] + v2[..., None, :]).reshape(B, S, PEER_HEADS, PEER_TOPK * PEER_TOPK)
    vs, pos = lax.top_k(cand, PEER_TOPK)
    e1 = jnp.take_along_axis(i1, pos // PEER_TOPK, axis=-1)
    e2 = jnp.take_along_axis(i2, pos % PEER_TOPK, axis=-1)
    idx = e1 * N_KEYS + e2
    g = jax.nn.softmax(vs, axis=-1)
    T = B * S
    nb = T // PEER_TOKEN_BLOCK

    def block(args):
        xb, ib, gb = args
        u = jnp.take(expert_down, ib, axis=0)
        a = jax.nn.gelu(jnp.einsum('td,thkd->thk', xb, u), approximate=False)
        w = (gb * a).astype(xb.dtype)
        vv = jnp.take(expert_up, ib, axis=0)
        return jnp.einsum('thk,thkd->td', w, vv)

    out = lax.map(block, (h.reshape(nb, PEER_TOKEN_BLOCK, D),
                          idx.reshape(nb, PEER_TOKEN_BLOCK, PEER_HEADS, PEER_TOPK),
                          g.reshape(nb, PEER_TOKEN_BLOCK, PEER_HEADS, PEER_TOPK)))
    return out.reshape(B, S, D)


def setup_inputs(seed: int = 0) -> dict:
    key = jax.random.key(seed)
    ks = jax.random.split(key, 24)
    n = jax.random.normal
    L = DEPTH
    in_cols = 3 * FOX_WIDTH + FOX_HEADS + 4 * MLSTM_WIDTH + 2 * MLSTM_HEADS
    return {
        "x": n(ks[0], (BATCH, SEQ, D_MODEL), jnp.float32),
        "c": n(ks[1], (BATCH, D_MODEL), jnp.float32),
        "w_ada": n(ks[2], (L, D_MODEL, N_MOD * D_MODEL), jnp.float32) * (0.5 * D_MODEL ** -0.5),
        "b_ada": n(ks[3], (L, N_MOD * D_MODEL), jnp.float32) * 0.02,
        "norm1_gain": 1.0 + 0.02 * n(ks[4], (L, D_MODEL), jnp.float32),
        "norm2_gain": 1.0 + 0.02 * n(ks[5], (L, D_MODEL), jnp.float32),
        "w_in": n(ks[6], (L, D_MODEL, in_cols), jnp.float32) * D_MODEL ** -0.5,
        "fox_f_bias": 3.0 + 0.1 * n(ks[7], (L, FOX_HEADS), jnp.float32),
        "fox_q_gain": 1.0 + 0.02 * n(ks[8], (L, FOX_HEAD_DIM), jnp.float32),
        "fox_k_gain": 1.0 + 0.02 * n(ks[9], (L, FOX_HEAD_DIM), jnp.float32),
        "mlstm_conv_w": n(ks[10], (L, CONV_K, 2 * MLSTM_WIDTH), jnp.float32) * CONV_K ** -0.5,
        "mlstm_conv_b": 0.02 * n(ks[11], (L, 2 * MLSTM_WIDTH), jnp.float32),
        "mlstm_i_bias": -1.0 + 0.1 * n(ks[12], (L, MLSTM_HEADS), jnp.float32),
        "mlstm_f_bias": 3.0 + 0.1 * n(ks[13], (L, MLSTM_HEADS), jnp.float32),
        "mlstm_head_gain": 1.0 + 0.02 * n(ks[14], (L, MLSTM_WIDTH), jnp.float32),
        "w_out": n(ks[15], (L, D_MIX, D_MODEL), jnp.float32) * D_MIX ** -0.5,
        "peer_w_query": n(ks[16], (L, D_MODEL, PEER_HEADS * PEER_QDIM), jnp.float32) * D_MODEL ** -0.5,
        "peer_sub_keys_1": n(ks[17], (L, N_KEYS, PEER_QDIM // 2), jnp.float32) * (PEER_QDIM // 2) ** -0.5,
        "peer_sub_keys_2": n(ks[18], (L, N_KEYS, PEER_QDIM // 2), jnp.float32) * (PEER_QDIM // 2) ** -0.5,
        "peer_expert_down": n(ks[19], (L, N_EXPERTS, D_MODEL), jnp.float32) * D_MODEL ** -0.5,
        "peer_expert_up": n(ks[20], (L, N_EXPERTS, D_MODEL), jnp.float32) * PEER_HEADS ** -0.5,
    }


def reference(x, c, w_ada, b_ada, norm1_gain, norm2_gain, w_in, fox_f_bias, fox_q_gain,
              fox_k_gain, mlstm_conv_w, mlstm_conv_b, mlstm_i_bias, mlstm_f_bias,
              mlstm_head_gain, w_out, peer_w_query, peer_sub_keys_1, peer_sub_keys_2,
              peer_expert_down, peer_expert_up):
    for l in range(DEPTH):
        mod = jax.nn.silu(c) @ w_ada[l] + b_ada[l]
        shift1, scale1, gate1, shift2, scale2, gate2 = jnp.split(mod[:, None, :], N_MOD, axis=-1)
        h = rms_norm(x, norm1_gain[l]) * (1.0 + scale1) + shift1
        y = mixer(h, w_in[l], fox_f_bias[l], fox_q_gain[l], fox_k_gain[l], mlstm_conv_w[l],
                  mlstm_conv_b[l], mlstm_i_bias[l], mlstm_f_bias[l], mlstm_head_gain[l], w_out[l])
        x = x + gate1 * y
        h = rms_norm(x, norm2_gain[l]) * (1.0 + scale2) + shift2
        y = peer(h, peer_w_query[l], peer_sub_keys_1[l], peer_sub_keys_2[l],
                 peer_expert_down[l], peer_expert_up[l])
        x = x + gate2 * y
    return x
```

```python
import functools

import jax
import jax.numpy as jnp
from jax import lax
from jax.experimental import pallas as pl
from jax.experimental.pallas import tpu as pltpu

F32 = jnp.float32
BF16 = jnp.bfloat16

FOX_HEADS = 8
FOX_HEAD_DIM = 128
MLSTM_HEADS = 4
MLSTM_HEAD_DIM = 256
CONV_K = 4
PEER_HEADS = 8
N_KEYS = 128
PEER_TOPK = 16
RMS_EPS = 1e-6
N_MOD = 6
GATE_COLS = 128
N_GATES = FOX_HEADS + 2 * MLSTM_HEADS
MLSTM_CHUNK = 256
NEG_BIG = -1e30

VMEM_LIMIT_BYTES = 56 * 1024 * 1024

NT_DIMS = (((1,), (1,)), ((), ()))
TN_DIMS = (((0,), (0,)), ((), ()))


def _params(sem):
    return pltpu.CompilerParams(dimension_semantics=sem, vmem_limit_bytes=VMEM_LIMIT_BYTES)


def _split_bf16(x):
    hi = x.astype(BF16)
    lo = (x - hi.astype(F32)).astype(BF16)
    return hi, lo


def _dot(a, b):
    return jnp.dot(a, b, preferred_element_type=F32)


def _dot3(a_hi, a_lo, b_hi, b_lo, dims=None):
    if dims is None:
        f = _dot
    else:
        f = lambda u, v: lax.dot_general(u, v, dims, preferred_element_type=F32)
    return f(a_hi, b_hi) + (f(a_lo, b_hi) + f(a_hi, b_lo))


def _ada_kernel(c_ref, w_ref, b_ref, o_ref):
    c = c_ref[...]
    sc = c * jax.nn.sigmoid(c)
    sc_hi, sc_lo = _split_bf16(sc)
    w_hi, w_lo = _split_bf16(w_ref[...])
    o_ref[...] = _dot3(sc_hi, sc_lo, w_hi, w_lo) + b_ref[...]


def _ada(c, w_ada, b_ada, tn=1024):
    bsz, d = c.shape
    n = w_ada.shape[1]
    return pl.pallas_call(
        _ada_kernel,
        grid=(n // tn,),
        in_specs=[
            pl.BlockSpec((bsz, d), lambda j: (0, 0)),
            pl.BlockSpec((d, tn), lambda j: (0, j)),
            pl.BlockSpec((1, tn), lambda j: (0, j)),
        ],
        out_specs=pl.BlockSpec((bsz, tn), lambda j: (0, j)),
        out_shape=jax.ShapeDtypeStruct((bsz, n), F32),
        compiler_params=_params(("arbitrary",)),
        name="ada",
    )(c, w_ada, b_ada.reshape(1, n))


def _modulated_norm(x, gain, scale, shift):
    r = lax.rsqrt(jnp.mean(x * x, axis=-1, keepdims=True) + RMS_EPS)
    return (x * r) * gain * (1.0 + scale) + shift


def _inproj_kernel(x_ref, gain_ref, mod_ref, w_ref, wg_hi_ref, wg_lo_ref,
                   p_ref, gate_ref, h_scr):
    @pl.when(pl.program_id(1) == 0)
    def _():
        h = _modulated_norm(x_ref[...], gain_ref[...], mod_ref[0, 1:2, :], mod_ref[0, 0:1, :])
        h_hi, h_lo = _split_bf16(h)
        h_scr[...] = h_hi
        gate_ref[...] = _dot3(h_hi, h_lo, wg_hi_ref[...], wg_lo_ref[...])

    p_ref[...] = _dot(h_scr[...], w_ref[...]).astype(p_ref.dtype)


def _inproj(x2, gain, mod3, w_main, wg_hi, wg_lo, seq, tm=1024, tn=1024):
    t, d = x2.shape
    n = w_main.shape[1]
    tm = min(tm, seq)
    per_batch = seq // tm
    return pl.pallas_call(
        _inproj_kernel,
        grid=(t // tm, n // tn),
        in_specs=[
            pl.BlockSpec((tm, d), lambda i, j: (i, 0)),
            pl.BlockSpec((1, d), lambda i, j: (0, 0)),
            pl.BlockSpec((1, N_MOD, d), lambda i, j: (i // per_batch, 0, 0)),
            pl.BlockSpec((d, tn), lambda i, j: (0, j)),
            pl.BlockSpec((d, GATE_COLS), lambda i, j: (0, 0)),
            pl.BlockSpec((d, GATE_COLS), lambda i, j: (0, 0)),
        ],
        out_specs=[
            pl.BlockSpec((tm, tn), lambda i, j: (i, j)),
            pl.BlockSpec((tm, GATE_COLS), lambda i, j: (i, 0)),
        ],
        out_shape=[
            jax.ShapeDtypeStruct((t, n), BF16),
            jax.ShapeDtypeStruct((t, GATE_COLS), F32),
        ],
        scratch_shapes=[pltpu.VMEM((tm, d), BF16)],
        compiler_params=_params(("arbitrary", "arbitrary")),
        name="inproj",
    )(x2, gain, mod3, w_main, wg_hi, wg_lo)


def _gates_kernel(g_ref, bias_ref, o_ref, *, seq):
    blk = MLSTM_CHUNK
    row = lax.broadcasted_iota(jnp.int32, (N_GATES, 1), 0)
    is_fox = row < FOX_HEADS
    is_inp = jnp.logical_and(row >= FOX_HEADS, row < FOX_HEADS + MLSTM_HEADS)
    tri = (lax.broadcasted_iota(jnp.int32, (blk, blk), 0)
           <= lax.broadcasted_iota(jnp.int32, (blk, blk), 1)).astype(BF16)
    bias = bias_ref[...]
    carry = jnp.zeros((N_GATES, 1), F32)
    for c in range(seq // blk):
        x = g_ref[0, :, c * blk:(c + 1) * blk] + bias
        ls = jnp.minimum(x, 0.0) - jnp.log1p(jnp.exp(-jnp.abs(x)))
        p0 = ls.astype(BF16)
        r1 = ls - p0.astype(F32)
        p1 = r1.astype(BF16)
        p2 = (r1 - p1.astype(F32)).astype(BF16)
        cs = _dot(p0, tri) + (_dot(p1, tri) + _dot(p2, tri))
        cs = cs + jnp.where(is_fox, carry, 0.0)
        carry = cs[:, blk - 1:blk]
        o_ref[0, :, c * blk:(c + 1) * blk] = jnp.where(is_inp, x, cs)


def _gates(g_t, bias):
    bsz, _, seq = g_t.shape
    return pl.pallas_call(
        functools.partial(_gates_kernel, seq=seq),
        grid=(bsz,),
        in_specs=[
            pl.BlockSpec((1, N_GATES, seq), lambda b: (b, 0, 0)),
            pl.BlockSpec((N_GATES, 1), lambda b: (0, 0)),
        ],
        out_specs=pl.BlockSpec((1, N_GATES, seq), lambda b: (b, 0, 0)),
        out_shape=jax.ShapeDtypeStruct((bsz, N_GATES, seq), F32),
        compiler_params=_params(("arbitrary",)),
        name="gates",
    )(g_t, bias)


def _head_norm(x, gain):
    r = lax.rsqrt(jnp.mean(x * x, axis=-1, keepdims=True) + RMS_EPS)
    return (x * r) * gain


def _fox_kernel(q_ref, k_ref, v_ref, fc_ref, fr_ref, qg_ref, kg_ref, o_ref, kn_scr, *, tq):
    qi = pl.program_id(1)

    @pl.when(qi == 0)
    def _():
        kn_scr[...] = _head_norm(k_ref[...].astype(F32), kg_ref[...]).astype(BF16)

    scale = FOX_HEAD_DIM ** -0.5
    qn = (_head_norm(q_ref[...].astype(F32), qg_ref[...]) * scale).astype(BF16)
    fq = fc_ref[0]

    def block(kb, carry, masked):
        m, l, acc = carry
        ks = pl.multiple_of(kb * tq, tq)
        s = lax.dot_general(qn, kn_scr[pl.ds(ks, tq), :], NT_DIMS, preferred_element_type=F32)
        s = s + (fq - fr_ref[0, kb])
        if masked:
            causal = (lax.broadcasted_iota(jnp.int32, (tq, tq), 0)
                      >= lax.broadcasted_iota(jnp.int32, (tq, tq), 1))
            s = jnp.where(causal, s, NEG_BIG)
        m_new = jnp.maximum(m, jnp.max(s, axis=-1, keepdims=True))
        alpha = jnp.exp(m - m_new)
        p = jnp.exp(s - m_new)
        l = alpha * l + jnp.sum(p, axis=-1, keepdims=True)
        acc = alpha * acc + _dot(p.astype(BF16), v_ref[pl.ds(ks, tq), :])
        return m_new, l, acc

    init = (jnp.full((tq, 1), NEG_BIG, F32), jnp.zeros((tq, 1), F32),
            jnp.zeros((tq, FOX_HEAD_DIM), F32))
    carry = lax.fori_loop(0, qi, lambda kb, c: block(kb, c, False), init)
    _, l, acc = block(qi, carry, True)
    o_ref[...] = (acc / l).astype(o_ref.dtype)


def _fox(p, f_col, f_row, q_gain, k_gain, bsz, seq, tq=256):
    t = p.shape[0]
    nq = seq // tq
    hd = FOX_HEAD_DIM
    return pl.pallas_call(
        functools.partial(_fox_kernel, tq=tq),
        grid=(bsz * FOX_HEADS, nq),
        in_specs=[
            pl.BlockSpec((tq, hd), lambda bh, qi: ((bh // FOX_HEADS) * nq + qi, bh % FOX_HEADS)),
            pl.BlockSpec((seq, hd), lambda bh, qi: (bh // FOX_HEADS, FOX_HEADS + bh % FOX_HEADS)),
            pl.BlockSpec((seq, hd), lambda bh, qi: (bh // FOX_HEADS, 2 * FOX_HEADS + bh % FOX_HEADS)),
            pl.BlockSpec((1, tq, 1), lambda bh, qi: (bh, qi, 0)),
            pl.BlockSpec((1, nq, 1, tq), lambda bh, qi: (bh, 0, 0, 0)),
            pl.BlockSpec((1, hd), lambda bh, qi: (0, 0)),
            pl.BlockSpec((1, hd), lambda bh, qi: (0, 0)),
        ],
        out_specs=pl.BlockSpec((tq, hd), lambda bh, qi: ((bh // FOX_HEADS) * nq + qi, bh % FOX_HEADS)),
        out_shape=jax.ShapeDtypeStruct((t, FOX_HEADS * hd), BF16),
        scratch_shapes=[pltpu.VMEM((seq, hd), BF16)],
        compiler_params=_params(("arbitrary", "arbitrary")),
        name="fox",
    )(p, p, p, f_col, f_row.reshape(bsz * FOX_HEADS, nq, 1, tq), q_gain, k_gain)


def _conv_silu(x, x_prev, w, b):
    row = lax.broadcasted_iota(jnp.int32, x.shape, 0)
    y = x * w[CONV_K - 1:CONV_K, :] + b
    for j in range(1, CONV_K):
        shifted = jnp.where(row < j, pltpu.roll(x_prev, j, 0), pltpu.roll(x, j, 0))
        y = y + shifted * w[CONV_K - 1 - j:CONV_K - j, :]
    return y * jax.nn.sigmoid(y)


def _mlstm_kernel(q_ref, k_ref, v_ref, og_ref, bc_ref, rc_ref, rr_ref,
                  wq_ref, wk_ref, bq_ref, bk_ref, hg_ref, out_ref,
                  c_scr, n_scr, m_scr, qprev_scr, kprev_scr):
    L = MLSTM_CHUNK
    dh = MLSTM_HEAD_DIM

    @pl.when(pl.program_id(1) == 0)
    def _():
        c_scr[...] = jnp.zeros_like(c_scr)
        n_scr[...] = jnp.zeros_like(n_scr)
        m_scr[...] = jnp.zeros_like(m_scr)
        qprev_scr[...] = jnp.zeros_like(qprev_scr)
        kprev_scr[...] = jnp.zeros_like(kprev_scr)

    xq = q_ref[...].astype(F32)
    xk = k_ref[...].astype(F32)
    q = _conv_silu(xq, qprev_scr[...], wq_ref[0], bq_ref[0])
    k = _conv_silu(xk, kprev_scr[...], wk_ref[0], bk_ref[0]) * (dh ** -0.5)
    qprev_scr[...] = xq
    kprev_scr[...] = xk
    qb = q.astype(BF16)
    vb = v_ref[...]

    b_col = bc_ref[0]
    r_col = rc_ref[0]
    r_row = rr_ref[0]
    m_prev = m_scr[...]

    causal = (lax.broadcasted_iota(jnp.int32, (L, L), 0)
              >= lax.broadcasted_iota(jnp.int32, (L, L), 1))
    log_d = jnp.where(causal, b_col + r_row, NEG_BIG)
    m_inter = b_col + m_prev
    m_t = jnp.maximum(m_inter, jnp.max(log_d, axis=-1, keepdims=True))
    d_m = jnp.exp(log_d - m_t)
    inter = jnp.exp(m_inter - m_t)
    w_m = lax.dot_general(qb, k.astype(BF16), NT_DIMS, preferred_element_type=F32) * d_m
    num = _dot(w_m.astype(BF16), vb) + inter * _dot(qb, c_scr[...].astype(BF16))
    den = (jnp.sum(w_m, axis=-1, keepdims=True)
           + inter * jnp.sum(q * n_scr[...], axis=-1, keepdims=True))
    h = num / jnp.maximum(jnp.abs(den), jnp.exp(-m_t))

    b_last = b_col[L - 1:L, :]
    m_new = jnp.maximum(b_last + m_prev, b_last + jnp.max(r_row, axis=-1, keepdims=True))
    decay = jnp.exp(b_last + m_prev - m_new)
    kw = k * jnp.exp(r_col + (b_last - m_new))
    c_scr[...] = decay * c_scr[...] + lax.dot_general(
        kw.astype(BF16), vb, TN_DIMS, preferred_element_type=F32)
    n_scr[...] = decay * n_scr[...] + jnp.sum(kw, axis=0, keepdims=True)
    m_scr[...] = m_new

    hn = _head_norm(h, hg_ref[0])
    out_ref[...] = (hn * jax.nn.sigmoid(og_ref[...].astype(F32))).astype(out_ref.dtype)


def _mlstm(p, b_col, r_col, r_row, conv_w, conv_b, head_gain, bsz, seq):
    t = p.shape[0]
    L = MLSTM_CHUNK
    dh = MLSTM_HEAD_DIM
    nh = MLSTM_HEADS
    nc = seq // L
    col0 = 3 * FOX_HEADS * FOX_HEAD_DIM // dh
    rowblk = lambda bh, c: (bh // nh) * nc + c
    conv_w_h = conv_w.reshape(CONV_K, 2 * nh, dh).transpose(1, 0, 2)
    pspec = lambda off: pl.BlockSpec((L, dh), lambda bh, c: (rowblk(bh, c), col0 + off * nh + bh % nh))
    return pl.pallas_call(
        _mlstm_kernel,
        grid=(bsz * nh, nc),
        in_specs=[
            pspec(0), pspec(1), pspec(2), pspec(3),
            pl.BlockSpec((1, L, 1), lambda bh, c: (bh, c, 0)),
            pl.BlockSpec((1, L, 1), lambda bh, c: (bh, c, 0)),
            pl.BlockSpec((1, 1, L), lambda bh, c: (bh * nc + c, 0, 0)),
            pl.BlockSpec((1, CONV_K, dh), lambda bh, c: (bh % nh, 0, 0)),
            pl.BlockSpec((1, CONV_K, dh), lambda bh, c: (nh + bh % nh, 0, 0)),
            pl.BlockSpec((1, 1, dh), lambda bh, c: (bh % nh, 0, 0)),
            pl.BlockSpec((1, 1, dh), lambda bh, c: (nh + bh % nh, 0, 0)),
            pl.BlockSpec((1, 1, dh), lambda bh, c: (bh % nh, 0, 0)),
        ],
        out_specs=pl.BlockSpec((L, dh), lambda bh, c: (rowblk(bh, c), bh % nh)),
        out_shape=jax.ShapeDtypeStruct((t, nh * dh), BF16),
        scratch_shapes=[
            pltpu.VMEM((dh, dh), F32),
            pltpu.VMEM((1, dh), F32),
            pltpu.VMEM((1, 1), F32),
            pltpu.VMEM((L, dh), F32),
            pltpu.VMEM((L, dh), F32),
        ],
        compiler_params=_params(("arbitrary", "arbitrary")),
        name="mlstm",
    )(p, p, p, p, b_col, r_col, r_row, conv_w_h, conv_w_h,
      conv_b.reshape(2 * nh, 1, dh), conv_b.reshape(2 * nh, 1, dh),
      head_gain.reshape(nh, 1, dh))


def _outproj_kernel(fox_ref, mls_ref, w_ref, x_ref, mod_ref, gain_ref, x1_ref, h2_ref):
    half = fox_ref.shape[1]
    y = _dot(fox_ref[...], w_ref[0:half, :]) + _dot(mls_ref[...], w_ref[half:, :])
    x1 = x_ref[...] + mod_ref[0, 2:3, :] * y
    x1_ref[...] = x1
    h2 = _modulated_norm(x1, gain_ref[...], mod_ref[0, 4:5, :], mod_ref[0, 3:4, :])
    h2_ref[...] = h2.astype(h2_ref.dtype)


def _outproj(fox, mls, w_out, x2, mod3, gain, seq, tm=256):
    t, d = x2.shape
    half = fox.shape[1]
    per_batch = seq // tm
    return pl.pallas_call(
        _outproj_kernel,
        grid=(t // tm,),
        in_specs=[
            pl.BlockSpec((tm, half), lambda i: (i, 0)),
            pl.BlockSpec((tm, half), lambda i: (i, 0)),
            pl.BlockSpec((2 * half, d), lambda i: (0, 0)),
            pl.BlockSpec((tm, d), lambda i: (i, 0)),
            pl.BlockSpec((1, N_MOD, d), lambda i: (i // per_batch, 0, 0)),
            pl.BlockSpec((1, d), lambda i: (0, 0)),
        ],
        out_specs=[
            pl.BlockSpec((tm, d), lambda i: (i, 0)),
            pl.BlockSpec((tm, d), lambda i: (i, 0)),
        ],
        out_shape=[
            jax.ShapeDtypeStruct((t, d), F32),
            jax.ShapeDtypeStruct((t, d), BF16),
        ],
        compiler_params=_params(("arbitrary",)),
        name="outproj",
    )(fox, mls, w_out, x2, mod3, gain)


def _candidate_pairs():
    return [(j, l) for j in range(PEER_TOPK) for l in range(PEER_TOPK)
            if (j + 1) * (l + 1) <= PEER_TOPK]


def _top_values(x, k):
    vals = []
    for _ in range(k):
        m = jnp.max(x, axis=0, keepdims=True)
        vals.append(m)
        x = jnp.where(x == m, -jnp.inf, x)
    return vals


def _route_kernel(h2_ref, wq_ref, k1_ref, k2_ref, s1p_ref, s2_ref, tau_ref,
                  cand_scr, cand2_scr):
    half = N_KEYS
    q = _dot(h2_ref[...], wq_ref[...])
    k1_hi, k1_lo = _split_bf16(k1_ref[...])
    k2_hi, k2_lo = _split_bf16(k2_ref[...])
    pairs = _candidate_pairs()
    n_pad = cand_scr.shape[0]
    tm = cand_scr.shape[1]
    cand_scr[len(pairs):, :] = jnp.full((n_pad - len(pairs), tm), -jnp.inf, F32)
    cand2_scr[len(pairs):, :] = jnp.full((n_pad - len(pairs), tm), jnp.inf, F32)
    for h in range(PEER_HEADS):
        qa_hi, qa_lo = _split_bf16(q[:, (2 * h) * half:(2 * h + 1) * half])
        qb_hi, qb_lo = _split_bf16(q[:, (2 * h + 1) * half:(2 * h + 2) * half])
        s1 = _dot3(k1_hi, k1_lo, qa_hi, qa_lo, NT_DIMS)
        s2 = _dot3(k2_hi, k2_lo, qb_hi, qb_lo, NT_DIMS)
        v1 = _top_values(s1, PEER_TOPK)
        v2 = _top_values(s2, PEER_TOPK)
        for r, (j, l) in enumerate(pairs):
            cand_scr[r:r + 1, :] = v1[j] + v2[l]
        cand = cand_scr[...]
        tau = _top_values(cand, PEER_TOPK)[-1]
        sel = cand >= tau
        cmax = v1[0] + v2[0]
        z = jnp.sum(jnp.where(sel, jnp.exp(cand - cmax), 0.0), axis=0, keepdims=True)
        lse = cmax + jnp.log(z)
        s1p = s1 - lse
        for r, (j, l) in enumerate(pairs):
            cand2_scr[r:r + 1, :] = v2[l] + (v1[j] - lse)
        tau_p = jnp.min(jnp.where(sel, cand2_scr[...], jnp.inf), axis=0, keepdims=True)
        s1p_ref[h] = s1p
        s2_ref[h] = s2
        tau_ref[h:h + 1, :] = tau_p


def _route(h2, wq, keys1, keys2, tm=256):
    t, d = h2.shape
    n_pad = 56
    return pl.pallas_call(
        _route_kernel,
        grid=(t // tm,),
        in_specs=[
            pl.BlockSpec((tm, d), lambda i: (i, 0)),
            pl.BlockSpec(wq.shape, lambda i: (0, 0)),
            pl.BlockSpec(keys1.shape, lambda i: (0, 0)),
            pl.BlockSpec(keys2.shape, lambda i: (0, 0)),
        ],
        out_specs=[
            pl.BlockSpec((PEER_HEADS, N_KEYS, tm), lambda i: (0, 0, i)),
            pl.BlockSpec((PEER_HEADS, N_KEYS, tm), lambda i: (0, 0, i)),
            pl.BlockSpec((PEER_HEADS, tm), lambda i: (0, i)),
        ],
        out_shape=[
            jax.ShapeDtypeStruct((PEER_HEADS, N_KEYS, t), F32),
            jax.ShapeDtypeStruct((PEER_HEADS, N_KEYS, t), F32),
            jax.ShapeDtypeStruct((PEER_HEADS, t), F32),
        ],
        scratch_shapes=[pltpu.VMEM((n_pad, tm), F32), pltpu.VMEM((n_pad, tm), F32)],
        compiler_params=_params(("arbitrary",)),
        name="route",
    )(h2, wq, keys1, keys2)


def _gelu(a):
    return 0.5 * a * (1.0 + lax.erf(a * (2.0 ** -0.5)))


def _peer_kernel(h2_ref, u_ref, v_ref, s1p_ref, s2_ref, tau_ref, x1_ref, mod_ref, out_ref,
                 at_scr, w_scr, *, ne1, lane_chunk):
    j = pl.program_id(1)
    tt = h2_ref.shape[0]

    @pl.when(j == 0)
    def _():
        out_ref[...] = jnp.zeros_like(out_ref)

    at_scr[...] = lax.dot_general(u_ref[...], h2_ref[...], NT_DIMS, preferred_element_type=F32)

    e1_base = pl.multiple_of(j * ne1, ne1)
    for lc in range(tt // lane_chunk):
        ls = slice(lc * lane_chunk, (lc + 1) * lane_chunk)
        s1_rows = [s1p_ref[h, pl.ds(e1_base, ne1), ls] for h in range(PEER_HEADS)]
        for i in range(ne1):
            rs = slice(i * N_KEYS, (i + 1) * N_KEYS)
            g = jnp.zeros((N_KEYS, lane_chunk), F32)
            for h in range(PEER_HEADS):
                c = s2_ref[h, :, ls] + s1_rows[h][i:i + 1, :]
                g = g + jnp.where(c >= tau_ref[h:h + 1, ls], jnp.exp(c), 0.0)
            w_scr[rs, ls] = (_gelu(at_scr[rs, ls]) * g).astype(BF16)
    out_ref[...] += lax.dot_general(w_scr[...], v_ref[...], TN_DIMS, preferred_element_type=F32)

    @pl.when(j == pl.num_programs(1) - 1)
    def _():
        out_ref[...] = x1_ref[...] + mod_ref[0, 5:6, :] * out_ref[...]


def _peer(h2, u, v, s1p, s2, tau, x1, mod3, seq, tt=512, lane_chunk=128):
    t, d = h2.shape
    n_exp = u.shape[0]
    ne1 = 8
    et = ne1 * N_KEYS
    per_batch = seq // tt
    return pl.pallas_call(
        functools.partial(_peer_kernel, ne1=ne1, lane_chunk=lane_chunk),
        grid=(t // tt, n_exp // et),
        in_specs=[
            pl.BlockSpec((tt, d), lambda i, j: (i, 0)),
            pl.BlockSpec((et, d), lambda i, j: (j, 0)),
            pl.BlockSpec((et, d), lambda i, j: (j, 0)),
            pl.BlockSpec((PEER_HEADS, N_KEYS, tt), lambda i, j: (0, 0, i)),
            pl.BlockSpec((PEER_HEADS, N_KEYS, tt), lambda i, j: (0, 0, i)),
            pl.BlockSpec((PEER_HEADS, tt), lambda i, j: (0, i)),
            pl.BlockSpec((tt, d), lambda i, j: (i, 0)),
            pl.BlockSpec((1, N_MOD, d), lambda i, j: (i // per_batch, 0, 0)),
        ],
        out_specs=pl.BlockSpec((tt, d), lambda i, j: (i, 0)),
        out_shape=jax.ShapeDtypeStruct((t, d), F32),
        scratch_shapes=[pltpu.VMEM((et, tt), F32), pltpu.VMEM((et, tt), BF16)],
        compiler_params=_params(("arbitrary", "arbitrary")),
        name="peer",
    )(h2, u, v, s1p, s2, tau, x1, mod3)


def _layer(x2, c, bsz, seq, w_ada, b_ada, norm1_gain, norm2_gain, w_in, fox_f_bias, fox_q_gain,
           fox_k_gain, mlstm_conv_w, mlstm_conv_b, mlstm_i_bias, mlstm_f_bias,
           mlstm_head_gain, w_out, peer_w_query, peer_sub_keys_1, peer_sub_keys_2,
           peer_expert_down, peer_expert_up):
    t, d = x2.shape
    fw = FOX_HEADS * FOX_HEAD_DIM
    mw = MLSTM_HEADS * MLSTM_HEAD_DIM
    L = MLSTM_CHUNK

    mod3 = _ada(c, w_ada, b_ada).reshape(bsz, N_MOD, d)

    g0 = 3 * fw
    m0 = g0 + FOX_HEADS
    g1 = m0 + 4 * mw
    w_main = jnp.concatenate([w_in[:, :g0], w_in[:, m0:g1]], axis=1).astype(BF16)
    w_gate = jnp.concatenate([w_in[:, g0:m0], w_in[:, g1:]], axis=1)
    w_gate = jnp.pad(w_gate, ((0, 0), (0, GATE_COLS - N_GATES)))
    wg_hi, wg_lo = _split_bf16(w_gate)
    p, gate_logits = _inproj(x2, norm1_gain.reshape(1, d), mod3, w_main, wg_hi, wg_lo, seq)

    g_t = gate_logits[:, :N_GATES].reshape(bsz, seq, N_GATES).transpose(0, 2, 1)
    bias = jnp.concatenate([fox_f_bias, mlstm_i_bias, mlstm_f_bias]).reshape(N_GATES, 1)
    cs = _gates(g_t, bias)

    f_row = cs[:, :FOX_HEADS, :].reshape(bsz * FOX_HEADS, 1, seq)
    f_col = f_row.reshape(bsz * FOX_HEADS, seq, 1)
    fox = _fox(p, f_col, f_row, fox_q_gain.reshape(1, -1), fox_k_gain.reshape(1, -1), bsz, seq)

    i_log = cs[:, FOX_HEADS:FOX_HEADS + MLSTM_HEADS, :]
    b_loc = cs[:, FOX_HEADS + MLSTM_HEADS:, :]
    r = (i_log - b_loc).reshape(bsz * MLSTM_HEADS, seq)
    b_loc = b_loc.reshape(bsz * MLSTM_HEADS, seq)
    mls = _mlstm(p, b_loc.reshape(-1, seq, 1), r.reshape(-1, seq, 1),
                 r.reshape(-1, 1, L), mlstm_conv_w, mlstm_conv_b, mlstm_head_gain, bsz, seq)

    x1, h2 = _outproj(fox, mls, w_out.astype(BF16), x2, mod3, norm2_gain.reshape(1, d), seq)

    s1p, s2, tau = _route(h2, peer_w_query.astype(BF16), peer_sub_keys_1, peer_sub_keys_2)
    return _peer(h2, peer_expert_down.astype(BF16), peer_expert_up.astype(BF16),
                 s1p, s2, tau, x1, mod3, seq)


def kernel(x, c, w_ada, b_ada, norm1_gain, norm2_gain, w_in, fox_f_bias, fox_q_gain, fox_k_gain, mlstm_conv_w, mlstm_conv_b, mlstm_i_bias, mlstm_f_bias, mlstm_head_gain, w_out, peer_w_query, peer_sub_keys_1, peer_sub_keys_2, peer_expert_down, peer_expert_up):
    bsz, seq, d = x.shape
    x2 = x.reshape(bsz * seq, d)
    for l in range(w_ada.shape[0]):
        x2 = _layer(x2, c, bsz, seq, w_ada[l], b_ada[l], norm1_gain[l], norm2_gain[l], w_in[l],
                    fox_f_bias[l], fox_q_gain[l], fox_k_gain[l], mlstm_conv_w[l],
                    mlstm_conv_b[l], mlstm_i_bias[l], mlstm_f_bias[l], mlstm_head_gain[l],
                    w_out[l], peer_w_query[l], peer_sub_keys_1[l], peer_sub_keys_2[l],
                    peer_expert_down[l], peer_expert_up[l])
    return x2.reshape(bsz, seq, d)
```
